```python
import math
import jax
import jax.numpy as jnp
from jax import lax
import numpy as np

D_MODEL = 4096
BATCH = 4
SEQ = 4096
DEPTH = 1

CHUNK = 64
Q_BLOCK = 128
EPS = 1e-6

DIFF_HEAD_DIM = 128
DIFF_V_DIM = 2 * DIFF_HEAD_DIM
DIFF_HEADS = D_MODEL // (4 * DIFF_HEAD_DIM)
DIFF_QK_COLS = DIFF_HEADS * 2 * DIFF_HEAD_DIM
DIFF_WIDTH = DIFF_HEADS * DIFF_V_DIM

HG_DK = 128
HG_DV = 128
HG_HEADS = D_MODEL // (2 * HG_DK)
HG_WIDTH = HG_HEADS * HG_DK
HG_V_WIDTH = HG_HEADS * HG_DV

IN_SPLITS = (DIFF_QK_COLS, DIFF_QK_COLS, DIFF_WIDTH, HG_WIDTH, HG_WIDTH, HG_V_WIDTH, HG_V_WIDTH)
IN_COLS = sum(IN_SPLITS)
MIX_WIDTH = DIFF_WIDTH + HG_V_WIDTH

MEM_LEN = 256
MEM_HEADS = 4
MEM_HEAD_DIM = 128
MEM_WIDTH = MEM_HEADS * MEM_HEAD_DIM

PEER_HEADS = 8
PEER_N_KEYS = 128
PEER_N_EXPERTS = PEER_N_KEYS * PEER_N_KEYS
PEER_QDIM = 256
PEER_HALF = PEER_QDIM // 2
PEER_TOPK = 16
PEER_TOKEN_BLOCK = 64

kernel_name = 'hybrid_diffattn_hgrn2_peer_block'


def rms_norm(x, gain):
    xf = x.astype(jnp.float32)
    y = xf * lax.rsqrt(jnp.mean(xf * xf, axis=-1, keepdims=True) + EPS)
    return (y * gain.astype(jnp.float32)).astype(x.dtype)


def chunk_causal_mask(q_pos, k_pos):
    return k_pos[None, :] < ((q_pos // CHUNK + 1) * CHUNK)[:, None]


def diff_attention(q, k, v, lam):
    B, S = q.shape[:2]
    nqb = S // Q_BLOCK
    qb = q.reshape(B, nqb, Q_BLOCK, DIFF_HEADS, 2, DIFF_HEAD_DIM).transpose(1, 0, 2, 3, 4, 5)
    k_pos = jnp.arange(S)
    scale = DIFF_HEAD_DIM ** -0.5

    def block(args):
        q_blk, blk_idx = args
        s = jnp.einsum('bqhmd,bkhmd->bhmqk', q_blk, k).astype(jnp.float32) * scale
        q_pos = blk_idx * Q_BLOCK + jnp.arange(Q_BLOCK)
        mask = chunk_causal_mask(q_pos, k_pos)
        p = jax.nn.softmax(jnp.where(mask, s, -jnp.inf), axis=-1)
        a = p[:, :, 0] - lam * p[:, :, 1]
        return jnp.einsum('bhqk,bkhe->bqhe', a, v)

    out = lax.map(block, (qb, jnp.arange(nqb)))
    return out.transpose(1, 0, 2, 3, 4).reshape(B, S, DIFF_HEADS, DIFF_V_DIM)


def hgrn2_chunkwise(q, k, v, log_f):
    B, S, H, DK = q.shape
    DV = v.shape[-1]
    nc = S // CHUNK

    def to_chunks(t):
        return t.reshape(B, nc, CHUNK, H, t.shape[-1]).transpose(1, 0, 3, 2, 4)

    causal = jnp.tril(jnp.ones((CHUNK, CHUNK), dtype=bool))

    def step(state, inp):
        q_c, k_c, v_c, g_c = inp
        b = jnp.cumsum(g_c, axis=2)
        inter = jnp.einsum('bhtk,bhkv->bhtv', q_c * jnp.exp(b), state)
        rel = b[:, :, :, None, :] - b[:, :, None, :, :]
        decay = jnp.exp(jnp.where(causal[:, :, None], rel, -jnp.inf))
        scores = jnp.einsum('bhtk,bhsk,bhtsk->bhts', q_c, k_c, decay)
        intra = jnp.einsum('bhts,bhsv->bhtv', scores, v_c)
        b_last = b[:, :, -1:, :]
        new_state = (state * jnp.exp(b_last)[:, :, 0, :, None]
                     + jnp.einsum('bhsk,bhsv->bhkv', k_c * jnp.exp(b_last - b), v_c))
        return new_state, inter + intra

    s0 = jnp.zeros((B, H, DK, DV), jnp.float32)
    _, o = lax.scan(step, s0, (to_chunks(q), to_chunks(k), to_chunks(v), to_chunks(log_f)))
    return o.transpose(1, 0, 3, 2, 4).reshape(B, S, H, DV)


def memory_cross_attention(h, mem_n, w_q, w_k, w_v, w_o, q_gain, k_gain):
    B, S, _ = h.shape
    M = mem_n.shape[1]
    q = rms_norm((h @ w_q).reshape(B, S, MEM_HEADS, MEM_HEAD_DIM), q_gain)
    k = rms_norm((mem_n @ w_k).reshape(B, M, MEM_HEADS, MEM_HEAD_DIM), k_gain)
    v = (mem_n @ w_v).reshape(B, M, MEM_HEADS, MEM_HEAD_DIM)
    s = jnp.einsum('bqhd,bmhd->bhqm', q, k).astype(jnp.float32) * (MEM_HEAD_DIM ** -0.5)
    p = jax.nn.softmax(s, axis=-1)
    o = jnp.einsum('bhqm,bmhd->bqhd', p, v.astype(jnp.float32)).reshape(B, S, MEM_WIDTH)
    return o.astype(h.dtype) @ w_o


def peer_ffn(h, w_q, sub_keys, u_tab, v_tab):
    B, S, D = h.shape
    q = (h @ w_q).reshape(B, S, PEER_HEADS, 2, PEER_HALF)
    sc = jnp.einsum('bshpd,hpnd->bshpn', q, sub_keys).astype(jnp.float32)
    half_s, half_i = lax.top_k(sc, PEER_TOPK)
    cand = half_s[..., 0, :, None] + half_s[..., 1, None, :]
    best_s, best_pos = lax.top_k(cand.reshape(B, S, PEER_HEADS, PEER_TOPK * PEER_TOPK), PEER_TOPK)
    i1 = jnp.take_along_axis(half_i[..., 0, :], best_pos // PEER_TOPK, axis=-1)
    i2 = jnp.take_along_axis(half_i[..., 1, :], best_pos % PEER_TOPK, axis=-1)
    experts = i1 * PEER_N_KEYS + i2
    gates = jax.nn.softmax(best_s, axis=-1)
    nb = (B * S) // PEER_TOKEN_BLOCK
    hk = PEER_HEADS * PEER_TOPK
    xs = (h.reshape(nb, PEER_TOKEN_BLOCK, D),
          experts.reshape(nb, PEER_TOKEN_BLOCK, hk),
          gates.reshape(nb, PEER_TOKEN_BLOCK, hk))

    def block(args):
        x_b, e_b, g_b = args
        u = jnp.take(u_tab, e_b, axis=0)
        a = jnp.einsum('td,tkd->tk', x_b, u).astype(jnp.float32)
        w = (jax.nn.gelu(a, approximate=False) * g_b).astype(v_tab.dtype)
        return jnp.einsum('tk,tkd->td', w, jnp.take(v_tab, e_b, axis=0))

    y = lax.map(block, xs)
    return y.reshape(B, S, D)


def setup_inputs(seed: int = 0) -> dict:
    key = jax.random.key(seed)
    ks = jax.random.split(key, 32)
    f32 = jnp.float32

    def nrm(k, shape, scale):
        return jax.random.normal(k, shape, f32) * scale

    def gain(k, n):
        return 1.0 + 0.02 * jax.random.normal(k, (DEPTH, n), f32)

    return {
        'x': nrm(ks[0], (BATCH, SEQ, D_MODEL), 1.0),
        'mem': nrm(ks[1], (BATCH, MEM_LEN, D_MODEL), 1.0),
        'hgrn_lb_logits': nrm(ks[2], (DEPTH + 1, HG_WIDTH), 0.5),
        'mix_norm': gain(ks[3], D_MODEL),
        'w_in': nrm(ks[4], (DEPTH, D_MODEL, IN_COLS), D_MODEL ** -0.5),
        'diff_q_norm': gain(ks[5], DIFF_HEAD_DIM),
        'diff_k_norm': gain(ks[6], DIFF_HEAD_DIM),
        'diff_lambda_q1': nrm(ks[7], (DEPTH, DIFF_HEAD_DIM), 0.1),
        'diff_lambda_k1': nrm(ks[8], (DEPTH, DIFF_HEAD_DIM), 0.1),
        'diff_lambda_q2': nrm(ks[9], (DEPTH, DIFF_HEAD_DIM), 0.1),
        'diff_lambda_k2': nrm(ks[10], (DEPTH, DIFF_HEAD_DIM), 0.1),
        'diff_subln': gain(ks[11], DIFF_V_DIM),
        'hgrn_out_norm': gain(ks[12], HG_DV),
        'w_out': nrm(ks[13], (DEPTH, MIX_WIDTH, D_MODEL), MIX_WIDTH ** -0.5),
        'mem_norm': gain(ks[14], D_MODEL),
        'mem_kv_norm': gain(ks[15], D_MODEL),
        'w_mem_q': nrm(ks[16], (DEPTH, D_MODEL, MEM_WIDTH), D_MODEL ** -0.5),
        'w_mem_k': nrm(ks[17], (DEPTH, D_MODEL, MEM_WIDTH), D_MODEL ** -0.5),
        'w_mem_v': nrm(ks[18], (DEPTH, D_MODEL, MEM_WIDTH), D_MODEL ** -0.5),
        'mem_q_norm': gain(ks[19], MEM_HEAD_DIM),
        'mem_k_norm': gain(ks[20], MEM_HEAD_DIM),
        'w_mem_o': nrm(ks[21], (DEPTH, MEM_WIDTH, D_MODEL), MEM_WIDTH ** -0.5),
        'ffn_norm': gain(ks[22], D_MODEL),
        'peer_w_q': nrm(ks[23], (DEPTH, D_MODEL, PEER_HEADS * PEER_QDIM), D_MODEL ** -0.5),
        'peer_sub_keys': nrm(ks[24], (DEPTH, PEER_HEADS, 2, PEER_N_KEYS, PEER_HALF), PEER_HALF ** -0.5),
        'peer_u': nrm(ks[25], (DEPTH, PEER_N_EXPERTS, D_MODEL), D_MODEL ** -0.5),
        'peer_v': nrm(ks[26], (DEPTH, PEER_N_EXPERTS, D_MODEL), 0.5),
    }


def reference(x, mem, hgrn_lb_logits, mix_norm, w_in, diff_q_norm, diff_k_norm,
              diff_lambda_q1, diff_lambda_k1, diff_lambda_q2, diff_lambda_k2, diff_subln,
              hgrn_out_norm, w_out, mem_norm, mem_kv_norm, w_mem_q, w_mem_k, w_mem_v,
              mem_q_norm, mem_k_norm, w_mem_o, ffn_norm, peer_w_q, peer_sub_keys,
              peer_u, peer_v):
    B, S, _ = x.shape
    f32 = jnp.float32
    split_points = [int(c) for c in np.cumsum(IN_SPLITS)[:-1]]
    lb_all = jnp.cumsum(jax.nn.softmax(hgrn_lb_logits.astype(f32), axis=0), axis=0)

    for l in range(DEPTH):
        h = rms_norm(x, mix_norm[l])
        proj = h @ w_in[l]
        dq, dk, dv, gq, gf, gi, gg = jnp.split(proj, split_points, axis=-1)

        lambda_init = 0.8 - 0.6 * math.exp(-0.3 * l)
        lam = (jnp.exp(jnp.sum(diff_lambda_q1[l].astype(f32) * diff_lambda_k1[l].astype(f32)))
               - jnp.exp(jnp.sum(diff_lambda_q2[l].astype(f32) * diff_lambda_k2[l].astype(f32)))
               + lambda_init)
        dq = rms_norm(dq.reshape(B, S, DIFF_HEADS, 2, DIFF_HEAD_DIM), diff_q_norm[l])
        dk = rms_norm(dk.reshape(B, S, DIFF_HEADS, 2, DIFF_HEAD_DIM), diff_k_norm[l])
        dv = dv.reshape(B, S, DIFF_HEADS, DIFF_V_DIM).astype(f32)
        d_out = diff_attention(dq, dk, dv, lam)
        d_out = rms_norm(d_out, diff_subln[l]) * (1.0 - lambda_init)

        lb = lb_all[l].reshape(HG_HEADS, HG_DK)
        gf32 = gf.astype(f32).reshape(B, S, HG_HEADS, HG_DK)
        log_f = jnp.log(lb + (1.0 - lb) * jax.nn.sigmoid(gf32))
        k_in = (1.0 - lb) * jax.nn.sigmoid(-gf32)
        q_hg = jax.nn.silu(gq.astype(f32)).reshape(B, S, HG_HEADS, HG_DK)
        v_hg = gi.astype(f32).reshape(B, S, HG_HEADS, HG_DV)
        o_hg = hgrn2_chunkwise(q_hg, k_in, v_hg, log_f)
        o_hg = rms_norm(o_hg, hgrn_out_norm[l]) * jax.nn.silu(gg.astype(f32).reshape(B, S, HG_HEADS, HG_DV))

        mixed = jnp.concatenate([d_out.reshape(B, S, DIFF_WIDTH),
                                 o_hg.reshape(B, S, HG_V_WIDTH)], axis=-1).astype(x.dtype)
        x = x + (mixed @ w_out[l]).astype(x.dtype)

        mem_n = rms_norm(mem, mem_kv_norm[l])
        x = x + memory_cross_attention(rms_norm(x, mem_norm[l]), mem_n, w_mem_q[l], w_mem_k[l],
                                       w_mem_v[l], w_mem_o[l], mem_q_norm[l], mem_k_norm[l]).astype(x.dtype)

        x = x + peer_ffn(rms_norm(x, ffn_norm[l]), peer_w_q[l], peer_sub_keys[l],
                         peer_u[l], peer_v[l]).astype(x.dtype)
    return x
```

```python
import functools
import math

import numpy as np
import jax
import jax.numpy as jnp
from jax import lax
from jax.experimental import pallas as pl
from jax.experimental.pallas import tpu as pltpu

F32 = jnp.float32
BF16 = jnp.bfloat16
EPS = 1e-6
LANES = 128
VMEM_LIMIT = 56 * 1024 * 1024

CHUNK = 64
HEAD_DIM = 128
DIFF_HEADS = 8
HG_HEADS = 16
MEM_HEADS = 4
PEER_HEADS = 8
PEER_KEYS = 128
PEER_TOPK = 16

_NT = (((1,), (1,)), ((), ()))
_TN = (((0,), (0,)), ((), ()))


def _params(*sem):
    return pltpu.CompilerParams(dimension_semantics=sem, vmem_limit_bytes=VMEM_LIMIT)


def _sigmoid(x):
    return 1.0 / (1.0 + jnp.exp(-x))


def _rmsnorm_kernel(x_ref, g_ref, o_ref):
    x = x_ref[...]
    ms = jnp.mean(x * x, axis=-1, keepdims=True)
    o_ref[...] = (x * lax.rsqrt(ms + EPS) * g_ref[...]).astype(o_ref.dtype)


def _rmsnorm(x, gain, tm=512):
    t, d = x.shape
    tm = min(tm, t)
    return pl.pallas_call(
        _rmsnorm_kernel,
        grid=(t // tm,),
        in_specs=[pl.BlockSpec((tm, d), lambda i: (i, 0)),
                  pl.BlockSpec((1, d), lambda i: (0, 0))],
        out_specs=pl.BlockSpec((tm, d), lambda i: (i, 0)),
        out_shape=jax.ShapeDtypeStruct((t, d), BF16),
        compiler_params=_params("parallel"),
        name="rmsnorm",
    )(x, gain.reshape(1, d).astype(F32))


def _mm_kernel(x_ref, w_ref, o_ref):
    o_ref[...] = jnp.dot(x_ref[...], w_ref[...], preferred_element_type=F32).astype(o_ref.dtype)


def _mm_groupnorm_kernel(x_ref, w_ref, g_ref, o_ref):
    acc = jnp.dot(x_ref[...], w_ref[...], preferred_element_type=F32)
    for c in range(0, acc.shape[1], HEAD_DIM):
        blk = acc[:, c:c + HEAD_DIM]
        ms = jnp.mean(blk * blk, axis=-1, keepdims=True)
        o_ref[:, c:c + HEAD_DIM] = (blk * lax.rsqrt(ms + EPS) * g_ref[:, c:c + HEAD_DIM]).astype(o_ref.dtype)


def _mm_residual_kernel(x_ref, w_ref, r_ref, o_ref):
    o_ref[...] = r_ref[...] + jnp.dot(x_ref[...], w_ref[...], preferred_element_type=F32)


def _matmul(x, w, out_dtype, *, gain=None, residual=None, tm=1024, bn=1024, name="matmul"):
    t, k = x.shape
    n = w.shape[1]
    tm, bn = min(tm, t), min(bn, n)
    in_specs = [pl.BlockSpec((tm, k), lambda i, j: (i, 0)),
                pl.BlockSpec((k, bn), lambda i, j: (0, j))]
    args = [x, w]
    if gain is not None:
        body = _mm_groupnorm_kernel
        in_specs.append(pl.BlockSpec((1, bn), lambda i, j: (0, j)))
        args.append(gain.reshape(1, n).astype(F32))
    elif residual is not None:
        body = _mm_residual_kernel
        in_specs.append(pl.BlockSpec((tm, bn), lambda i, j: (i, j)))
        args.append(residual)
    else:
        body = _mm_kernel
    return pl.pallas_call(
        body,
        grid=(t // tm, n // bn),
        in_specs=in_specs,
        out_specs=pl.BlockSpec((tm, bn), lambda i, j: (i, j)),
        out_shape=jax.ShapeDtypeStruct((t, n), out_dtype),
        compiler_params=_params("parallel", "parallel"),
        name=name,
    )(*args)


def _diff_attn_kernel(lq1_ref, lk1_ref, lq2_ref, lk2_ref, sub_ref, q_ref, k_ref, v_ref, o_ref,
                      m_sc, l_sc, acc_sc, *, tq, lambda_init):
    qi = pl.program_id(2)
    dh = HEAD_DIM
    m_sc[...] = jnp.full(m_sc.shape, -jnp.inf, F32)
    l_sc[...] = jnp.zeros(l_sc.shape, F32)
    acc_sc[...] = jnp.zeros(acc_sc.shape, F32)
    q = q_ref[0]

    def step(j, masked):
        rows = pl.ds(pl.multiple_of(j * tq, tq), tq)
        ks = k_ref[0, rows, :]
        vs = v_ref[0, rows, :]
        for m in range(2):
            s = lax.dot_general(q[:, m * dh:(m + 1) * dh], ks[:, m * dh:(m + 1) * dh], _NT,
                                preferred_element_type=F32)
            if masked:
                q_pos = qi * tq + lax.broadcasted_iota(jnp.int32, (tq, 1), 0)
                k_pos = j * tq + lax.broadcasted_iota(jnp.int32, (1, tq), 1)
                s = jnp.where(k_pos < (q_pos // CHUNK + 1) * CHUNK, s, -jnp.inf)
            m_prev = m_sc[m]
            m_new = jnp.maximum(m_prev, jnp.max(s, axis=-1, keepdims=True))
            alpha = jnp.exp(m_prev - m_new)
            p = jnp.exp(s - m_new)
            l_sc[m] = alpha * l_sc[m] + jnp.sum(p, axis=-1, keepdims=True)
            acc_sc[m] = alpha * acc_sc[m] + jnp.dot(p.astype(BF16), vs, preferred_element_type=F32)
            m_sc[m] = m_new

    def unmasked(j, carry):
        step(j, False)
        return carry

    lax.fori_loop(0, qi, unmasked, 0)
    step(qi, True)

    lam = (jnp.exp(jnp.sum(lq1_ref[...] * lk1_ref[...], axis=-1, keepdims=True))
           - jnp.exp(jnp.sum(lq2_ref[...] * lk2_ref[...], axis=-1, keepdims=True)) + lambda_init)
    o = acc_sc[0] / l_sc[0] - lam * (acc_sc[1] / l_sc[1])
    ms = jnp.mean(o * o, axis=-1, keepdims=True)
    o_ref[0] = (o * lax.rsqrt(ms + EPS) * sub_ref[...] * (1.0 - lambda_init)).astype(o_ref.dtype)


def _diff_attention(qk, v, lam_vecs, subln, lambda_init, *, tq=512):
    b, s, _ = v.shape
    dv = 2 * HEAD_DIM
    tq = min(tq, s)
    vec = pl.BlockSpec((1, HEAD_DIM), lambda bi, h, i: (0, 0))
    return pl.pallas_call(
        functools.partial(_diff_attn_kernel, tq=tq, lambda_init=lambda_init),
        grid=(b, DIFF_HEADS, s // tq),
        in_specs=[vec, vec, vec, vec,
                  pl.BlockSpec((1, dv), lambda bi, h, i: (0, 0)),
                  pl.BlockSpec((1, tq, dv), lambda bi, h, i: (bi, i, h)),
                  pl.BlockSpec((1, s, dv), lambda bi, h, i: (bi, 0, DIFF_HEADS + h)),
                  pl.BlockSpec((1, s, dv), lambda bi, h, i: (bi, 0, h))],
        out_specs=pl.BlockSpec((1, tq, dv), lambda bi, h, i: (bi, i, h)),
        out_shape=jax.ShapeDtypeStruct((b, s, DIFF_HEADS * dv), BF16),
        scratch_shapes=[pltpu.VMEM((2, tq, 1), F32), pltpu.VMEM((2, tq, 1), F32),
                        pltpu.VMEM((2, tq, dv), F32)],
        compiler_params=_params("parallel", "parallel", "parallel"),
        name="diff_attention",
    )(*[u.reshape(1, HEAD_DIM).astype(F32) for u in lam_vecs], subln.reshape(1, dv).astype(F32), qk, qk, v)


_HGRN_LEVELS = (32, 16, 8, 4, 2, 1)


def _hgrn_constants():
    c = CHUNK
    t = np.arange(c)[:, None]
    s = np.arange(c)[None, :]
    mats = [(s <= t).astype(np.float32)]
    masks = [(s == t).astype(np.float32)]
    for m in _HGRN_LEVELS:
        mid = (t // (2 * m)) * (2 * m) + m
        upper = t >= mid
        mats.append(np.where(upper, (s >= mid) & (s <= t), (s > t) & (s < mid)).astype(np.float32))
        masks.append(((t // (2 * m) == s // (2 * m)) & (t % (2 * m) >= m) & (s % (2 * m) < m)).astype(np.float32))
    cmat = np.concatenate(mats, axis=0)
    return np.concatenate([cmat, cmat, cmat], axis=1), np.stack(masks)


def _hgrn_kernel(lb_ref, onorm_ref, cmat_ref, mask_ref, gq_ref, gf_ref, gi_ref, gg_ref, o_ref, st_ref):
    c = CHUNK

    @pl.when(pl.program_id(1) == 0)
    def _():
        st_ref[...] = jnp.zeros(st_ref.shape, F32)

    def head(h, carry):
        sl = pl.ds(pl.multiple_of(h * HEAD_DIM, HEAD_DIM), HEAD_DIM)
        logits = lb_ref[:, sl]
        e = jnp.exp(logits - jnp.max(logits, axis=0, keepdims=True))
        lb = e[0:1] / jnp.sum(e, axis=0, keepdims=True)
        gf = gf_ref[0, :, sl]
        gq = gq_ref[0, :, sl]
        gg = gg_ref[0, :, sl]
        vb = gi_ref[0, :, sl].astype(BF16)
        sg = _sigmoid(gf)
        g = jnp.log(lb + (1.0 - lb) * sg)
        kk = (1.0 - lb) * (1.0 - sg)
        q = gq * _sigmoid(gq)

        g_hi = g.astype(BF16)
        r1 = g - g_hi.astype(F32)
        g_mid = r1.astype(BF16)
        g_lo = (r1 - g_mid.astype(F32)).astype(BF16)
        g3 = jnp.concatenate([g_hi, g_mid, g_lo], axis=0)
        e_all = jnp.dot(cmat_ref[...], g3, preferred_element_type=F32)
        b = e_all[0:c]
        b_last = b[c - 1:c]

        st = st_ref[h]
        inter = lax.dot_general((q * jnp.exp(b)).astype(BF16), st.astype(BF16), _NT,
                                preferred_element_type=F32)
        scores = jnp.zeros((c, c), F32)
        for lvl in range(len(_HGRN_LEVELS) + 1):
            if lvl == 0:
                ql, kl = q, kk
            else:
                x = jnp.exp(e_all[lvl * c:(lvl + 1) * c])
                ql, kl = q * x, kk * x
            sc = lax.dot_general(ql.astype(BF16), kl.astype(BF16), _NT, preferred_element_type=F32)
            scores = scores + sc * mask_ref[lvl]
        o = inter + jnp.dot(scores.astype(BF16), vb, preferred_element_type=F32)

        k_hat = (kk * jnp.exp(b_last - b)).astype(BF16)
        st_ref[h] = st * jnp.exp(b_last) + lax.dot_general(vb, k_hat, _TN, preferred_element_type=F32)

        ms = jnp.mean(o * o, axis=-1, keepdims=True)
        o_ref[0, :, sl] = (o * lax.rsqrt(ms + EPS) * onorm_ref[...] * (gg * _sigmoid(gg))).astype(o_ref.dtype)
        return carry

    lax.fori_loop(0, HG_HEADS, head, 0)


def _hgrn2(proj, lb_logits, out_norm):
    b, s, w4 = proj.shape
    w = w4 // 4
    cmat, masks = _hgrn_constants()
    col = lambda k: pl.BlockSpec((1, CHUNK, w), lambda bi, ci: (bi, ci, k))
    return pl.pallas_call(
        _hgrn_kernel,
        grid=(b, s // CHUNK),
        in_specs=[pl.BlockSpec(lb_logits.shape, lambda bi, ci: (0, 0)),
                  pl.BlockSpec((1, HEAD_DIM), lambda bi, ci: (0, 0)),
                  pl.BlockSpec(cmat.shape, lambda bi, ci: (0, 0)),
                  pl.BlockSpec(masks.shape, lambda bi, ci: (0, 0, 0)),
                  col(0), col(1), col(2), col(3)],
        out_specs=pl.BlockSpec((1, CHUNK, w), lambda bi, ci: (bi, ci, 0)),
        out_shape=jax.ShapeDtypeStruct((b, s, w), BF16),
        scratch_shapes=[pltpu.VMEM((HG_HEADS, HEAD_DIM, HEAD_DIM), F32)],
        compiler_params=_params("parallel", "arbitrary"),
        name="hgrn2",
    )(lb_logits.astype(F32), out_norm.reshape(1, HEAD_DIM).astype(F32),
      jnp.asarray(cmat, BF16), jnp.asarray(masks, F32), proj, proj, proj, proj)


def _mem_attn_kernel(q_ref, k_ref, v_ref, wo_ref, x_ref, o_ref):
    q = q_ref[0]
    k = k_ref[0]
    v = v_ref[0]
    heads = []
    for h in range(MEM_HEADS):
        sl = slice(h * HEAD_DIM, (h + 1) * HEAD_DIM)
        s = lax.dot_general(q[:, sl], k[:, sl], _NT, preferred_element_type=F32)
        p = jnp.exp(s - jnp.max(s, axis=-1, keepdims=True))
        o = jnp.dot(p.astype(BF16), v[:, sl], preferred_element_type=F32)
        heads.append((o / jnp.sum(p, axis=-1, keepdims=True)).astype(BF16))
    o_ref[0] = x_ref[0] + jnp.dot(jnp.concatenate(heads, axis=-1), wo_ref[...], preferred_element_type=F32)


def _mem_attention(q, k, v, w_o, x, *, tm=512):
    b, s, d = x.shape
    mw = q.shape[-1]
    m = k.shape[1]
    tm = min(tm, s)
    return pl.pallas_call(
        _mem_attn_kernel,
        grid=(b, s // tm),
        in_specs=[pl.BlockSpec((1, tm, mw), lambda bi, i: (bi, i, 0)),
                  pl.BlockSpec((1, m, mw), lambda bi, i: (bi, 0, 0)),
                  pl.BlockSpec((1, m, mw), lambda bi, i: (bi, 0, 0)),
                  pl.BlockSpec((mw, d), lambda bi, i: (0, 0)),
                  pl.BlockSpec((1, tm, d), lambda bi, i: (bi, i, 0))],
        out_specs=pl.BlockSpec((1, tm, d), lambda bi, i: (bi, i, 0)),
        out_shape=jax.ShapeDtypeStruct((b, s, d), F32),
        compiler_params=_params("parallel", "parallel"),
        name="mem_attention",
    )(q, k, v, w_o, x)


_PEER_ROUNDS = PEER_TOPK + 1
_PEER_LIST = 24


def _peer_route_kernel(q_ref, keys_ref, thr_ref, coef_ref, s2_ref, e2_ref):
    tb = q_ref.shape[0]
    neg = -jnp.inf
    rows = lax.broadcasted_iota(jnp.int32, (_PEER_LIST, tb), 0)

    def top_values(work, rounds):
        out = []
        for _ in range(rounds):
            m = jnp.max(work, axis=0, keepdims=True)
            out.append(m)
            work = jnp.where(work == m, neg, work)
        return out

    for h in range(PEER_HEADS):
        halves, lists = [], []
        for p in range(2):
            c0 = (2 * h + p) * HEAD_DIM
            sc = lax.dot_general(keys_ref[h, p], q_ref[:, c0:c0 + HEAD_DIM], _NT,
                                 preferred_element_type=F32)
            halves.append(sc)
            lst = jnp.full((_PEER_LIST, tb), neg, F32)
            for r, m in enumerate(top_values(sc, _PEER_ROUNDS)):
                lst = jnp.where(rows == r, m, lst)
            lists.append(lst)
        t1, t2 = lists
        cand = jnp.concatenate([t1[0:1] + t2] + [t1[r:r + 1] + t2[0:8] for r in range(1, 8)]
                               + [t1[8:] + t2[0:1]], axis=0)
        best = top_values(cand, _PEER_ROUNDS)
        tau = 0.5 * (best[PEER_TOPK - 1] + best[PEER_TOPK])
        top = best[0]
        z = jnp.sum(jnp.where(cand >= tau, jnp.exp(cand - top), 0.0), axis=0, keepdims=True)
        s1, s2 = halves
        thr_ref[h] = tau - s1
        coef_ref[h] = jnp.exp(s1 - t1[0:1]) / z
        s2_ref[h] = s2
        e2_ref[h] = jnp.exp(s2 - t2[0:1])


def _peer_route(q, sub_keys, *, tb=256):
    t, w = q.shape
    tb = min(tb, t)
    out = jax.ShapeDtypeStruct((PEER_HEADS, PEER_KEYS, t), F32)
    spec = pl.BlockSpec((PEER_HEADS, PEER_KEYS, tb), lambda i: (0, 0, i))
    return pl.pallas_call(
        _peer_route_kernel,
        grid=(t // tb,),
        in_specs=[pl.BlockSpec((tb, w), lambda i: (i, 0)),
                  pl.BlockSpec(sub_keys.shape, lambda i: (0, 0, 0, 0))],
        out_specs=[spec, spec, spec, spec],
        out_shape=[out, out, out, out],
        compiler_params=_params("parallel"),
        name="peer_route",
    )(q, sub_keys)


_THR_ROWS = 8


def _peer_dense_kernel(h_ref, u_ref, v_ref, thr_ref, coef_ref, s2_ref, e2_ref, x_ref, o_ref, *, rows):
    j = pl.program_id(1)

    @pl.when(j == 0)
    def _():
        o_ref[...] = x_ref[...]

    a_t = lax.dot_general(u_ref[...], h_ref[...], _NT, preferred_element_type=F32)
    base = (j % (_THR_ROWS // rows)) * rows
    w_rows = []
    for r in range(rows):
        a = a_t[r * PEER_KEYS:(r + 1) * PEER_KEYS]
        gate = jnp.zeros(a.shape, F32)
        for h in range(PEER_HEADS):
            thr = thr_ref[h, pl.ds(base + r, 1), :]
            coef = coef_ref[h, pl.ds(base + r, 1), :]
            gate = gate + jnp.where(s2_ref[h] >= thr, e2_ref[h] * coef, 0.0)
        gelu = 0.5 * a * (1.0 + lax.erf(a * (2.0 ** -0.5)))
        w_rows.append((gelu * gate).astype(BF16))
    w_t = jnp.concatenate(w_rows, axis=0)
    o_ref[...] += lax.dot_general(w_t, v_ref[...], _TN, preferred_element_type=F32)


def _peer_dense(h, u, v, thr, coef, s2, e2, x, *, tb=512, eb=512):
    t, d = h.shape
    n = u.shape[0]
    tb = min(tb, t)
    rows = eb // PEER_KEYS
    once = pl.Buffered(1)
    route = pl.BlockSpec((PEER_HEADS, _THR_ROWS, tb), lambda i, j: (0, j // (_THR_ROWS // rows), i))
    half = pl.BlockSpec((PEER_HEADS, PEER_KEYS, tb), lambda i, j: (0, 0, i), pipeline_mode=once)
    return pl.pallas_call(
        functools.partial(_peer_dense_kernel, rows=rows),
        grid=(t // tb, n // eb),
        in_specs=[pl.BlockSpec((tb, d), lambda i, j: (i, 0), pipeline_mode=once),
                  pl.BlockSpec((eb, d), lambda i, j: (j, 0)),
                  pl.BlockSpec((eb, d), lambda i, j: (j, 0)),
                  route, route, half, half,
                  pl.BlockSpec((tb, d), lambda i, j: (i, 0), pipeline_mode=once)],
        out_specs=pl.BlockSpec((tb, d), lambda i, j: (i, 0)),
        out_shape=jax.ShapeDtypeStruct((t, d), F32),
        compiler_params=_params("parallel", "arbitrary"),
        name="peer_dense",
    )(h, u, v, thr, coef, s2, e2, x)


def kernel(x, mem, hgrn_lb_logits, mix_norm, w_in, diff_q_norm, diff_k_norm, diff_lambda_q1, diff_lambda_k1,
           diff_lambda_q2, diff_lambda_k2, diff_subln, hgrn_out_norm, w_out, mem_norm, mem_kv_norm, w_mem_q,
           w_mem_k, w_mem_v, mem_q_norm, mem_k_norm, w_mem_o, ffn_norm, peer_w_q, peer_sub_keys, peer_u, peer_v):
    assert w_in.shape[0] == 1, "single-layer block"
    b, s, d = x.shape
    t = b * s
    lyr = 0
    lambda_init = 0.8 - 0.6 * math.exp(-0.3 * lyr)
    scale = HEAD_DIM ** -0.5
    qk_cols = 2 * DIFF_HEADS * 2 * HEAD_DIM
    dv_cols = DIFF_HEADS * 2 * HEAD_DIM
    x2d = x.reshape(t, d)

    w_in_b = w_in[lyr].astype(BF16)
    h = _rmsnorm(x2d, mix_norm[lyr])
    reps = qk_cols // 2 // HEAD_DIM
    qk_gain = jnp.concatenate([jnp.tile(diff_q_norm[lyr] * scale, reps), jnp.tile(diff_k_norm[lyr], reps)])
    qk = _matmul(h, w_in_b[:, :qk_cols], BF16, gain=qk_gain, name="in_proj_qk")
    dv = _matmul(h, w_in_b[:, qk_cols:qk_cols + dv_cols], BF16, name="in_proj_v")
    hg = _matmul(h, w_in_b[:, qk_cols + dv_cols:], F32, name="in_proj_hgrn")
    d_out = _diff_attention(qk.reshape(b, s, -1), dv.reshape(b, s, -1),
                            (diff_lambda_q1[lyr], diff_lambda_k1[lyr], diff_lambda_q2[lyr], diff_lambda_k2[lyr]),
                            diff_subln[lyr], lambda_init)
    o_hg = _hgrn2(hg.reshape(b, s, -1), hgrn_lb_logits, hgrn_out_norm[lyr])
    mixed = jnp.concatenate([d_out, o_hg], axis=-1).reshape(t, -1)
    x1 = _matmul(mixed, w_out[lyr].astype(BF16), F32, residual=x2d, name="out_proj")

    m_len = mem.shape[1]
    mem_n = _rmsnorm(mem.reshape(b * m_len, d), mem_kv_norm[lyr])
    mk = _matmul(mem_n, w_mem_k[lyr].astype(BF16), BF16, gain=jnp.tile(mem_k_norm[lyr], MEM_HEADS), name="mem_k")
    mv = _matmul(mem_n, w_mem_v[lyr].astype(BF16), BF16, name="mem_v")
    hq = _rmsnorm(x1, mem_norm[lyr])
    mq = _matmul(hq, w_mem_q[lyr].astype(BF16), BF16, gain=jnp.tile(mem_q_norm[lyr] * scale, MEM_HEADS),
                 name="mem_q")
    x2 = _mem_attention(mq.reshape(b, s, -1), mk.reshape(b, m_len, -1), mv.reshape(b, m_len, -1),
                        w_mem_o[lyr].astype(BF16), x1.reshape(b, s, d)).reshape(t, d)

    h3 = _rmsnorm(x2, ffn_norm[lyr])
    pq = _matmul(h3, peer_w_q[lyr].astype(BF16), BF16, name="peer_q")
    thr, coef, s2, e2 = _peer_route(pq, peer_sub_keys[lyr].astype(BF16))
    out = _peer_dense(h3, peer_u[lyr].astype(BF16), peer_v[lyr].astype(BF16), thr, coef, s2, e2, x2)
    return out.reshape(b, s, d)
```

```python
import functools
import math

import numpy as np
import jax
import jax.numpy as jnp
from jax import lax
from jax.experimental import pallas as pl
from jax.experimental.pallas import tpu as pltpu

F32 = jnp.float32
BF16 = jnp.bfloat16
EPS = 1e-6
LANES = 128
VMEM_LIMIT = 56 * 1024 * 1024

CHUNK = 64
HEAD_DIM = 128
DIFF_HEADS = 8
HG_HEADS = 16
MEM_HEADS = 4
PEER_HEADS = 8
PEER_KEYS = 128
PEER_TOPK = 16

_NT = (((1,), (1,)), ((), ()))
_TN = (((0,), (0,)), ((), ()))


def _params(*sem):
    return pltpu.CompilerParams(dimension_semantics=sem, vmem_limit_bytes=VMEM_LIMIT)


def _sigmoid(x):
    return 1.0 / (1.0 + jnp.exp(-x))


def _rmsnorm_kernel(x_ref, g_ref, o_ref):
    x = x_ref[...]
    ms = jnp.mean(x * x, axis=-1, keepdims=True)
    o_ref[...] = (x * lax.rsqrt(ms + EPS) * g_ref[...]).astype(o_ref.dtype)


def _rmsnorm(x, gain, tm=512):
    t, d = x.shape
    tm = min(tm, t)
    return pl.pallas_call(
        _rmsnorm_kernel,
        grid=(t // tm,),
        in_specs=[pl.BlockSpec((tm, d), lambda i: (i, 0)),
                  pl.BlockSpec((1, d), lambda i: (0, 0))],
        out_specs=pl.BlockSpec((tm, d), lambda i: (i, 0)),
        out_shape=jax.ShapeDtypeStruct((t, d), BF16),
        compiler_params=_params("parallel"),
        name="rmsnorm",
    )(x, gain.reshape(1, d).astype(F32))


def _mm_kernel(x_ref, w_ref, o_ref):
    o_ref[...] = jnp.dot(x_ref[...], w_ref[...], preferred_element_type=F32).astype(o_ref.dtype)


def _mm_groupnorm_kernel(x_ref, w_ref, g_ref, o_ref):
    acc = jnp.dot(x_ref[...], w_ref[...], preferred_element_type=F32)
    for c in range(0, acc.shape[1], HEAD_DIM):
        blk = acc[:, c:c + HEAD_DIM]
        ms = jnp.mean(blk * blk, axis=-1, keepdims=True)
        o_ref[:, c:c + HEAD_DIM] = (blk * lax.rsqrt(ms + EPS) * g_ref[:, c:c + HEAD_DIM]).astype(o_ref.dtype)


def _mm_residual_kernel(x_ref, w_ref, r_ref, o_ref):
    o_ref[...] = r_ref[...] + jnp.dot(x_ref[...], w_ref[...], preferred_element_type=F32)


def _matmul(x, w, out_dtype, *, gain=None, residual=None, tm=1024, bn=1024, name="matmul"):
    t, k = x.shape
    n = w.shape[1]
    tm, bn = min(tm, t), min(bn, n)
    in_specs = [pl.BlockSpec((tm, k), lambda i, j: (i, 0)),
                pl.BlockSpec((k, bn), lambda i, j: (0, j))]
    args = [x, w]
    if gain is not None:
        body = _mm_groupnorm_kernel
        in_specs.append(pl.BlockSpec((1, bn), lambda i, j: (0, j)))
        args.append(gain.reshape(1, n).astype(F32))
    elif residual is not None:
        body = _mm_residual_kernel
        in_specs.append(pl.BlockSpec((tm, bn), lambda i, j: (i, j)))
        args.append(residual)
    else:
        body = _mm_kernel
    return pl.pallas_call(
        body,
        grid=(t // tm, n // bn),
        in_specs=in_specs,
        out_specs=pl.BlockSpec((tm, bn), lambda i, j: (i, j)),
        out_shape=jax.ShapeDtypeStruct((t, n), out_dtype),
        compiler_params=_params("parallel", "parallel"),
        name=name,
    )(*args)


def _diff_attn_kernel(lq1_ref, lk1_ref, lq2_ref, lk2_ref, sub_ref, q_ref, k_ref, v_ref, o_ref,
                      s_sc, m_sc, l_sc, acc_sc, *, tq, lambda_init):
    qi = pl.program_id(2)
    dh = HEAD_DIM
    q = q_ref[0]
    m_sc[...] = jnp.full(m_sc.shape, -jnp.inf, F32)
    l_sc[...] = jnp.zeros(l_sc.shape, F32)
    acc_sc[...] = jnp.zeros(acc_sc.shape, F32)

    def lane_fold(x, op):
        out = x[:, 0:LANES]
        for c in range(LANES, x.shape[1], LANES):
            out = op(out, x[:, c:c + LANES])
        return out

    def score_step(j, masked):
        rows = pl.ds(pl.multiple_of(j * tq, tq), tq)
        for m in range(2):
            s = lax.dot_general(q[:, m * dh:(m + 1) * dh], k_ref[0, rows, m * dh:(m + 1) * dh], _NT,
                                preferred_element_type=F32)
            if masked:
                q_pos = qi * tq + lax.broadcasted_iota(jnp.int32, (tq, 1), 0)
                k_pos = j * tq + lax.broadcasted_iota(jnp.int32, (1, tq), 1)
                s = jnp.where(k_pos < (q_pos // CHUNK + 1) * CHUNK, s, -jnp.inf)
            s_sc[m, :, rows] = s
            m_sc[m] = jnp.maximum(m_sc[m], lane_fold(s, jnp.maximum))

    def unmasked(j, carry):
        score_step(j, False)
        return carry

    lax.fori_loop(0, qi, unmasked, 0)
    score_step(qi, True)

    row_max = [jnp.max(m_sc[m], axis=-1, keepdims=True) for m in range(2)]

    def value_step(j, carry):
        rows = pl.ds(pl.multiple_of(j * tq, tq), tq)
        vs = v_ref[0, rows, :]
        for m in range(2):
            p = jnp.exp2(s_sc[m, :, rows] - row_max[m])
            l_sc[m] += lane_fold(p, jnp.add)
            acc_sc[m] += jnp.dot(p.astype(BF16), vs, preferred_element_type=F32)
        return carry

    lax.fori_loop(0, qi + 1, value_step, 0)

    lam = (jnp.exp(jnp.sum(lq1_ref[...] * lk1_ref[...], axis=-1, keepdims=True))
           - jnp.exp(jnp.sum(lq2_ref[...] * lk2_ref[...], axis=-1, keepdims=True)) + lambda_init)
    row_sum = [jnp.sum(l_sc[m], axis=-1, keepdims=True) for m in range(2)]
    o = acc_sc[0] / row_sum[0] - lam * (acc_sc[1] / row_sum[1])
    ms = jnp.mean(o * o, axis=-1, keepdims=True)
    o_ref[0] = (o * lax.rsqrt(ms + EPS) * sub_ref[...] * (1.0 - lambda_init)).astype(o_ref.dtype)


def _diff_attention(qk, v, lam_vecs, subln, lambda_init, *, tq=512):
    b, s, _ = v.shape
    dv = 2 * HEAD_DIM
    tq = min(tq, s)
    vec = pl.BlockSpec((1, HEAD_DIM), lambda bi, h, i: (0, 0))
    return pl.pallas_call(
        functools.partial(_diff_attn_kernel, tq=tq, lambda_init=lambda_init),
        grid=(b, DIFF_HEADS, s // tq),
        in_specs=[vec, vec, vec, vec,
                  pl.BlockSpec((1, dv), lambda bi, h, i: (0, 0)),
                  pl.BlockSpec((1, tq, dv), lambda bi, h, i: (bi, i, h)),
                  pl.BlockSpec((1, s, dv), lambda bi, h, i: (bi, 0, DIFF_HEADS + h)),
                  pl.BlockSpec((1, s, dv), lambda bi, h, i: (bi, 0, h))],
        out_specs=pl.BlockSpec((1, tq, dv), lambda bi, h, i: (bi, i, h)),
        out_shape=jax.ShapeDtypeStruct((b, s, DIFF_HEADS * dv), BF16),
        scratch_shapes=[pltpu.VMEM((2, tq, s), F32), pltpu.VMEM((2, tq, LANES), F32),
                        pltpu.VMEM((2, tq, LANES), F32), pltpu.VMEM((2, tq, dv), F32)],
        compiler_params=_params("parallel", "parallel", "parallel"),
        name="diff_attention",
    )(*[u.reshape(1, HEAD_DIM).astype(F32) for u in lam_vecs], subln.reshape(1, dv).astype(F32), qk, qk, v)


_HGRN_LEVELS = (32, 16, 8, 4, 2, 1)


def _hgrn_constants():
    c = CHUNK
    t = np.arange(c)[:, None]
    s = np.arange(c)[None, :]
    mats = [(s <= t).astype(np.float32)]
    masks = [(s == t).astype(np.float32)]
    for m in _HGRN_LEVELS:
        mid = (t // (2 * m)) * (2 * m) + m
        upper = t >= mid
        mats.append(np.where(upper, (s >= mid) & (s <= t), (s > t) & (s < mid)).astype(np.float32))
        masks.append(((t // (2 * m) == s // (2 * m)) & (t % (2 * m) >= m) & (s % (2 * m) < m)).astype(np.float32))
    cmat = np.concatenate(mats, axis=0)
    return np.concatenate([cmat, cmat, cmat], axis=1), np.stack(masks)


def _hgrn_kernel(lb_ref, onorm_ref, cmat_ref, mask_ref, gq_ref, gf_ref, gi_ref, gg_ref, o_ref, st_ref):
    c = CHUNK

    @pl.when(pl.program_id(1) == 0)
    def _():
        st_ref[...] = jnp.zeros(st_ref.shape, F32)

    logits = lb_ref[...]
    e = jnp.exp(logits - jnp.max(logits, axis=0, keepdims=True))
    lb = e[0:1] / jnp.sum(e, axis=0, keepdims=True)
    gq = gq_ref[0]
    gg = gg_ref[0]
    sg = _sigmoid(gf_ref[0])
    g = jnp.log(lb + (1.0 - lb) * sg)
    kk = (1.0 - lb) * (1.0 - sg)
    q = gq * _sigmoid(gq)
    vb = gi_ref[0].astype(BF16)

    g_hi = g.astype(BF16)
    r1 = g - g_hi.astype(F32)
    g_mid = r1.astype(BF16)
    g_lo = (r1 - g_mid.astype(F32)).astype(BF16)
    e_all = jnp.dot(cmat_ref[...], jnp.concatenate([g_hi, g_mid, g_lo], axis=0),
                    preferred_element_type=F32)
    b = e_all[0:c]
    b_last = b[c - 1:c]
    q_lvl = [q.astype(BF16)]
    k_lvl = [kk.astype(BF16)]
    for lvl in range(1, len(_HGRN_LEVELS) + 1):
        x = jnp.exp(e_all[lvl * c:(lvl + 1) * c])
        q_lvl.append((q * x).astype(BF16))
        k_lvl.append((kk * x).astype(BF16))
    q_in = (q * jnp.exp(b)).astype(BF16)
    k_hat = (kk * jnp.exp(b_last - b)).astype(BF16)
    decay = jnp.exp(b_last)
    out_gate = onorm_ref[...] * (gg * _sigmoid(gg))

    heads = [slice(h * HEAD_DIM, (h + 1) * HEAD_DIM) for h in range(HG_HEADS)]
    scores = []
    for sl in heads:
        acc = jnp.zeros((c, c), F32)
        for lvl in range(len(_HGRN_LEVELS) + 1):
            sc = lax.dot_general(q_lvl[lvl][:, sl], k_lvl[lvl][:, sl], _NT, preferred_element_type=F32)
            acc = acc + sc * mask_ref[lvl]
        scores.append(acc.astype(BF16))
    states = [st_ref[h] for h in range(HG_HEADS)]
    outs = [lax.dot_general(q_in[:, sl], st.astype(BF16), _NT, preferred_element_type=F32)
            + jnp.dot(sc, vb[:, sl], preferred_element_type=F32)
            for sl, st, sc in zip(heads, states, scores)]
    for h, (sl, st) in enumerate(zip(heads, states)):
        st_ref[h] = st * decay[:, sl] + lax.dot_general(vb[:, sl], k_hat[:, sl], _TN,
                                                        preferred_element_type=F32)
    for sl, o in zip(heads, outs):
        ms = jnp.mean(o * o, axis=-1, keepdims=True)
        o_ref[0, :, sl] = (o * lax.rsqrt(ms + EPS) * out_gate[:, sl]).astype(o_ref.dtype)


def _hgrn2(proj, lb_logits, out_norm):
    b, s, w4 = proj.shape
    w = w4 // 4
    cmat, masks = _hgrn_constants()
    col = lambda k: pl.BlockSpec((1, CHUNK, w), lambda bi, ci: (bi, ci, k))
    return pl.pallas_call(
        _hgrn_kernel,
        grid=(b, s // CHUNK),
        in_specs=[pl.BlockSpec(lb_logits.shape, lambda bi, ci: (0, 0)),
                  pl.BlockSpec((1, w), lambda bi, ci: (0, 0)),
                  pl.BlockSpec(cmat.shape, lambda bi, ci: (0, 0)),
                  pl.BlockSpec(masks.shape, lambda bi, ci: (0, 0, 0)),
                  col(0), col(1), col(2), col(3)],
        out_specs=pl.BlockSpec((1, CHUNK, w), lambda bi, ci: (bi, ci, 0)),
        out_shape=jax.ShapeDtypeStruct((b, s, w), BF16),
        scratch_shapes=[pltpu.VMEM((HG_HEADS, HEAD_DIM, HEAD_DIM), F32)],
        compiler_params=_params("parallel", "arbitrary"),
        name="hgrn2",
    )(lb_logits.astype(F32), jnp.tile(out_norm.astype(F32), HG_HEADS).reshape(1, w),
      jnp.asarray(cmat, BF16), jnp.asarray(masks, F32), proj, proj, proj, proj)


def _mem_attn_kernel(q_ref, k_ref, v_ref, wo_ref, x_ref, o_ref):
    q = q_ref[0]
    k = k_ref[0]
    v = v_ref[0]
    heads = []
    for h in range(MEM_HEADS):
        sl = slice(h * HEAD_DIM, (h + 1) * HEAD_DIM)
        s = lax.dot_general(q[:, sl], k[:, sl], _NT, preferred_element_type=F32)
        p = jnp.exp(s - jnp.max(s, axis=-1, keepdims=True))
        o = jnp.dot(p.astype(BF16), v[:, sl], preferred_element_type=F32)
        heads.append((o / jnp.sum(p, axis=-1, keepdims=True)).astype(BF16))
    o_ref[0] = x_ref[0] + jnp.dot(jnp.concatenate(heads, axis=-1), wo_ref[...], preferred_element_type=F32)


def _mem_attention(q, k, v, w_o, x, *, tm=512):
    b, s, d = x.shape
    mw = q.shape[-1]
    m = k.shape[1]
    tm = min(tm, s)
    return pl.pallas_call(
        _mem_attn_kernel,
        grid=(b, s // tm),
        in_specs=[pl.BlockSpec((1, tm, mw), lambda bi, i: (bi, i, 0)),
                  pl.BlockSpec((1, m, mw), lambda bi, i: (bi, 0, 0)),
                  pl.BlockSpec((1, m, mw), lambda bi, i: (bi, 0, 0)),
                  pl.BlockSpec((mw, d), lambda bi, i: (0, 0)),
                  pl.BlockSpec((1, tm, d), lambda bi, i: (bi, i, 0))],
        out_specs=pl.BlockSpec((1, tm, d), lambda bi, i: (bi, i, 0)),
        out_shape=jax.ShapeDtypeStruct((b, s, d), F32),
        compiler_params=_params("parallel", "parallel"),
        name="mem_attention",
    )(q, k, v, w_o, x)


_PEER_ROUNDS = PEER_TOPK + 1
_PEER_LIST = 24


def _peer_route_kernel(q_ref, keys_ref, thr_ref, coef_ref, s2_ref, e2_ref):
    tb = q_ref.shape[0]
    neg = -jnp.inf
    rows = lax.broadcasted_iota(jnp.int32, (_PEER_LIST, tb), 0)

    def top_values(work, rounds):
        out = []
        for _ in range(rounds):
            m = jnp.max(work, axis=0, keepdims=True)
            out.append(m)
            work = jnp.where(work == m, neg, work)
        return out

    for h in range(PEER_HEADS):
        halves, lists = [], []
        for p in range(2):
            c0 = (2 * h + p) * HEAD_DIM
            sc = lax.dot_general(keys_ref[h, p], q_ref[:, c0:c0 + HEAD_DIM], _NT,
                                 preferred_element_type=F32)
            halves.append(sc)
            lst = jnp.full((_PEER_LIST, tb), neg, F32)
            for r, m in enumerate(top_values(sc, _PEER_ROUNDS)):
                lst = jnp.where(rows == r, m, lst)
            lists.append(lst)
        t1, t2 = lists
        cand = jnp.concatenate([t1[0:1] + t2] + [t1[r:r + 1] + t2[0:8] for r in range(1, 8)]
                               + [t1[8:] + t2[0:1]], axis=0)
        best = top_values(cand, _PEER_ROUNDS)
        tau = 0.5 * (best[PEER_TOPK - 1] + best[PEER_TOPK])
        top = best[0]
        z = jnp.sum(jnp.where(cand >= tau, jnp.exp(cand - top), 0.0), axis=0, keepdims=True)
        s1, s2 = halves
        thr_ref[h] = tau - s1
        coef_ref[h] = jnp.exp(s1 - t1[0:1]) / z
        s2_ref[h] = s2
        e2_ref[h] = jnp.exp(s2 - t2[0:1])


def _peer_route(q, sub_keys, *, tb=256):
    t, w = q.shape
    tb = min(tb, t)
    out = jax.ShapeDtypeStruct((PEER_HEADS, PEER_KEYS, t), F32)
    spec = pl.BlockSpec((PEER_HEADS, PEER_KEYS, tb), lambda i: (0, 0, i))
    return pl.pallas_call(
        _peer_route_kernel,
        grid=(t // tb,),
        in_specs=[pl.BlockSpec((tb, w), lambda i: (i, 0)),
                  pl.BlockSpec(sub_keys.shape, lambda i: (0, 0, 0, 0))],
        out_specs=[spec, spec, spec, spec],
        out_shape=[out, out, out, out],
        compiler_params=_params("parallel"),
        name="peer_route",
    )(q, sub_keys)


_THR_ROWS = 8


def _peer_dense_kernel(h_ref, u_ref, v_ref, thr_ref, coef_ref, s2_ref, e2_ref, x_ref, o_ref, *, rows):
    j = pl.program_id(1)

    @pl.when(j == 0)
    def _():
        o_ref[...] = x_ref[...]

    a_t = lax.dot_general(u_ref[...], h_ref[...], _NT, preferred_element_type=F32)
    base = (j % (_THR_ROWS // rows)) * rows
    w_rows = []
    for r in range(rows):
        a = a_t[r * PEER_KEYS:(r + 1) * PEER_KEYS]
        gate = jnp.zeros(a.shape, F32)
        for h in range(PEER_HEADS):
            thr = thr_ref[h, pl.ds(base + r, 1), :]
            coef = coef_ref[h, pl.ds(base + r, 1), :]
            gate = gate + jnp.where(s2_ref[h] >= thr, e2_ref[h] * coef, 0.0)
        gelu = 0.5 * a * (1.0 + lax.erf(a * (2.0 ** -0.5)))
        w_rows.append((gelu * gate).astype(BF16))
    w_t = jnp.concatenate(w_rows, axis=0)
    o_ref[...] += lax.dot_general(w_t, v_ref[...], _TN, preferred_element_type=F32)


def _peer_dense(h, u, v, thr, coef, s2, e2, x, *, tb=512, eb=512):
    t, d = h.shape
    n = u.shape[0]
    tb = min(tb, t)
    rows = eb // PEER_KEYS
    once = pl.Buffered(1)
    route = pl.BlockSpec((PEER_HEADS, _THR_ROWS, tb), lambda i, j: (0, j // (_THR_ROWS // rows), i))
    half = pl.BlockSpec((PEER_HEADS, PEER_KEYS, tb), lambda i, j: (0, 0, i), pipeline_mode=once)
    return pl.pallas_call(
        functools.partial(_peer_dense_kernel, rows=rows),
        grid=(t // tb, n // eb),
        in_specs=[pl.BlockSpec((tb, d), lambda i, j: (i, 0), pipeline_mode=once),
                  pl.BlockSpec((eb, d), lambda i, j: (j, 0)),
                  pl.BlockSpec((eb, d), lambda i, j: (j, 0)),
                  route, route, half, half,
                  pl.BlockSpec((tb, d), lambda i, j: (i, 0), pipeline_mode=once)],
        out_specs=pl.BlockSpec((tb, d), lambda i, j: (i, 0)),
        out_shape=jax.ShapeDtypeStruct((t, d), F32),
        compiler_params=_params("parallel", "arbitrary"),
        name="peer_dense",
    )(h, u, v, thr, coef, s2, e2, x)


def kernel(x, mem, hgrn_lb_logits, mix_norm, w_in, diff_q_norm, diff_k_norm, diff_lambda_q1, diff_lambda_k1,
           diff_lambda_q2, diff_lambda_k2, diff_subln, hgrn_out_norm, w_out, mem_norm, mem_kv_norm, w_mem_q,
           w_mem_k, w_mem_v, mem_q_norm, mem_k_norm, w_mem_o, ffn_norm, peer_w_q, peer_sub_keys, peer_u, peer_v):
    assert w_in.shape[0] == 1, "single-layer block"
    b, s, d = x.shape
    t = b * s
    lyr = 0
    lambda_init = 0.8 - 0.6 * math.exp(-0.3 * lyr)
    scale = HEAD_DIM ** -0.5
    qk_cols = 2 * DIFF_HEADS * 2 * HEAD_DIM
    dv_cols = DIFF_HEADS * 2 * HEAD_DIM
    x2d = x.reshape(t, d)

    w_in_b = w_in[lyr].astype(BF16)
    h = _rmsnorm(x2d, mix_norm[lyr])
    reps = qk_cols // 2 // HEAD_DIM
    qk_gain = jnp.concatenate([jnp.tile(diff_q_norm[lyr] * (scale * math.log2(math.e)), reps),
                               jnp.tile(diff_k_norm[lyr], reps)])
    qk = _matmul(h, w_in_b[:, :qk_cols], BF16, gain=qk_gain, name="in_proj_qk")
    dv = _matmul(h, w_in_b[:, qk_cols:qk_cols + dv_cols], BF16, name="in_proj_v")
    hg = _matmul(h, w_in_b[:, qk_cols + dv_cols:], F32, name="in_proj_hgrn")
    d_out = _diff_attention(qk.reshape(b, s, -1), dv.reshape(b, s, -1),
                            (diff_lambda_q1[lyr], diff_lambda_k1[lyr], diff_lambda_q2[lyr], diff_lambda_k2[lyr]),
                            diff_subln[lyr], lambda_init)
    o_hg = _hgrn2(hg.reshape(b, s, -1), hgrn_lb_logits, hgrn_out_norm[lyr])
    mixed = jnp.concatenate([d_out, o_hg], axis=-1).reshape(t, -1)
    x1 = _matmul(mixed, w_out[lyr].astype(BF16), F32, residual=x2d, name="out_proj")

    m_len = mem.shape[1]
    mem_n = _rmsnorm(mem.reshape(b * m_len, d), mem_kv_norm[lyr])
    mk = _matmul(mem_n, w_mem_k[lyr].astype(BF16), BF16, gain=jnp.tile(mem_k_norm[lyr], MEM_HEADS), name="mem_k")
    mv = _matmul(mem_n, w_mem_v[lyr].astype(BF16), BF16, name="mem_v")
    hq = _rmsnorm(x1, mem_norm[lyr])
    mq = _matmul(hq, w_mem_q[lyr].astype(BF16), BF16, gain=jnp.tile(mem_q_norm[lyr] * scale, MEM_HEADS),
                 name="mem_q")
    x2 = _mem_attention(mq.reshape(b, s, -1), mk.reshape(b, m_len, -1), mv.reshape(b, m_len, -1),
                        w_mem_o[lyr].astype(BF16), x1.reshape(b, s, d)).reshape(t, d)

    h3 = _rmsnorm(x2, ffn_norm[lyr])
    pq = _matmul(h3, peer_w_q[lyr].astype(BF16), BF16, name="peer_q")
    thr, coef, s2, e2 = _peer_route(pq, peer_sub_keys[lyr].astype(BF16))
    out = _peer_dense(h3, peer_u[lyr].astype(BF16), peer_v[lyr].astype(BF16), thr, coef, s2, e2, x2)
    return out.reshape(b, s, d)
```
